```python
import math
import jax, jax.numpy as jnp
from jax import lax
import numpy as np

D_MODEL = 1024
BATCH = 8
SEQ = 2048
DEPTH = 2
DEC_BATCH = 128
DEC_SEQ = 8
PAST_LEN = 8192
PAGE_SIZE = 128

H_A = 8
KV_A = 4
HD_A = 64
ROT_A = HD_A // 4
H_IDX = 4
D_IDX = 64
ROT_IDX = D_IDX // 4
TOPK_MAX = 256
H_B = 4
DK_B = 128
DV_B = 128
RET_CHUNK = 128
RET_THETA = 10000.0
H_C = 16
DN_C = 64
DR_C = 32
DV_C = 64
D_CQ = 384
D_C = 256
D_FF = 2816
CONV_W = 3
ROPE_THETA = 500000.0
EPS = 1e-6
QBLK = 128
N_EVEN = (DEPTH + 1) // 2
N_ODD = DEPTH // 2
EVEN_SIZES = (H_A * HD_A, KV_A * HD_A, KV_A * HD_A, H_IDX * D_IDX, D_IDX, H_IDX,
              H_B * DK_B, H_B * DK_B, H_B * DV_B, H_B * DV_B)
W_EVEN = sum(EVEN_SIZES)
W_ODD = D_CQ + D_C + DR_C

kernel_name = 'hybrid_dsa_retention_mla_convffn_step'


def rmsnorm(x, g):
    xf = x.astype(jnp.float32)
    y = xf * lax.rsqrt(jnp.mean(xf * xf, -1, keepdims=True) + EPS)
    return (y * g.astype(jnp.float32)).astype(x.dtype)


def rope(x, pos, rot, theta):
    half = rot // 2
    inv = theta ** (-jnp.arange(half, dtype=jnp.float32) * 2.0 / rot)
    ang = pos.astype(jnp.float32)[:, None] * inv[None, :]
    cos = jnp.cos(ang)[None, :, None, :]
    sin = jnp.sin(ang)[None, :, None, :]
    xf = x.astype(jnp.float32)
    x1, x2 = xf[..., :half], xf[..., half:rot]
    out = jnp.concatenate([x1 * cos - x2 * sin, x2 * cos + x1 * sin, xf[..., rot:]], -1)
    return out.astype(x.dtype)


def _split(z, sizes):
    out, o = [], 0
    for s in sizes:
        out.append(z[..., o:o + s])
        o += s
    return out


def _blocks(a, nb):
    b = a.shape[0]
    return jnp.moveaxis(a.reshape((b, nb, a.shape[1] // nb) + a.shape[2:]), 1, 0)


def _take_rows(rows, idx):
    return jax.vmap(lambda r, i: r[i])(rows, idx)


def even_project(h, w_in, pos):
    b, t, _ = h.shape
    qa, ka, va, qi, ki, wi, qb, kb, vb, gb = _split(h @ w_in, EVEN_SIZES)
    qa = rope(qa.reshape(b, t, H_A, HD_A), pos, ROT_A, ROPE_THETA)
    ka = rope(ka.reshape(b, t, KV_A, HD_A), pos, ROT_A, ROPE_THETA)
    va = va.reshape(b, t, KV_A, HD_A)
    qi = rope(qi.reshape(b, t, H_IDX, D_IDX), pos, ROT_IDX, ROPE_THETA)
    ki = rope(ki.reshape(b, t, 1, D_IDX), pos, ROT_IDX, ROPE_THETA)[:, :, 0]
    qb = rope(qb.reshape(b, t, H_B, DK_B), pos, DK_B, RET_THETA)
    kb = rope(kb.reshape(b, t, H_B, DK_B), pos, DK_B, RET_THETA) * (DK_B ** -0.5)
    vb = vb.reshape(b, t, H_B, DV_B)
    return (qa, ka, va, qi, ki, wi), (qb, kb, vb, gb)


def dsa_attend(qa, qi, wi, qpos, kidx, gather_kv):
    b, q = qa.shape[:2]
    n_keys = kidx.shape[1]
    topk = min(TOPK_MAX, n_keys // 4)
    rel = jax.nn.relu(jnp.einsum('bqhd,bkd->bqhk', qi, kidx, preferred_element_type=jnp.float32))
    score = jnp.einsum('bqhk,bqh->bqk', rel, wi.astype(jnp.float32)) * ((H_IDX * D_IDX) ** -0.5)
    admissible = jnp.arange(n_keys)[None, :] <= qpos[:, None]
    score = jnp.where(admissible[None], score, -jnp.inf)
    _, idx = lax.top_k(score, topk)
    valid = idx <= qpos[None, :, None]
    ks, vs = gather_kv(idx)
    qg = qa.reshape(b, q, KV_A, H_A // KV_A, HD_A)
    logits = jnp.einsum('bqngd,bqknd->bqngk', qg, ks, preferred_element_type=jnp.float32) * (HD_A ** -0.5)
    logits = jnp.where(valid[:, :, None, None, :], logits, -jnp.inf)
    p = jax.nn.softmax(logits, -1)
    o = jnp.einsum('bqngk,bqknd->bqngd', p.astype(vs.dtype), vs)
    return o.reshape(b, q, H_A * HD_A)


def dsa_prompt(qa, ka, va, qi, ki, wi, pos):
    b, t = qa.shape[:2]
    nb = t // QBLK
    gather = lambda idx: (_take_rows(ka, idx), _take_rows(va, idx))

    def one(args):
        q_, qi_, wi_, p_ = args
        return dsa_attend(q_, qi_, wi_, p_, ki, gather)

    o = lax.map(one, (_blocks(qa, nb), _blocks(qi, nb), _blocks(wi, nb), pos.reshape(nb, QBLK)))
    return jnp.moveaxis(o, 0, 1).reshape(b, t, H_A * HD_A)


def gather_paged(pool, li, page_table, new, idx):
    n_new = new.shape[1]
    ip = jnp.minimum(idx, PAST_LEN - 1)
    phys = jax.vmap(lambda pt, i: pt[i])(page_table, ip // PAGE_SIZE)
    past_rows = pool[li, phys, ip % PAGE_SIZE]
    new_rows = _take_rows(new, jnp.clip(idx - PAST_LEN, 0, n_new - 1))
    is_past = (idx < PAST_LEN).reshape(idx.shape + (1,) * (past_rows.ndim - idx.ndim))
    return jnp.where(is_past, past_rows, new_rows.astype(past_rows.dtype))


def dsa_sample(qa, ka, va, qi, ki, wi, pos, pool_k, pool_v, pool_ki, li, page_table):
    db = qa.shape[0]
    past_ki = pool_ki[li, page_table].reshape(db, PAST_LEN, D_IDX)
    kidx = jnp.concatenate([past_ki, ki.astype(past_ki.dtype)], 1)
    gather = lambda idx: (gather_paged(pool_k, li, page_table, ka, idx),
                          gather_paged(pool_v, li, page_table, va, idx))
    return dsa_attend(qa, qi, wi, pos, kidx, gather)


def retention(q, k, v, state0):
    b, t, h, _ = q.shape
    c = math.gcd(t, RET_CHUNK)
    n = t // c
    lg = jnp.log1p(-(2.0 ** (-5.0 - jnp.arange(h, dtype=jnp.float32))))
    i = jnp.arange(c, dtype=jnp.float32)
    diff = i[:, None] - i[None, :]
    dmat = jnp.where(diff >= 0, jnp.exp(lg[:, None, None] * jnp.maximum(diff, 0.0)), 0.0)
    xi = jnp.exp(lg[None, :] * (i[:, None] + 1.0))
    zeta = jnp.exp(lg[None, :] * (c - 1.0 - i[:, None]))
    g_chunk = jnp.exp(lg * c)
    to_chunks = lambda a: _blocks(a.astype(jnp.float32), n)

    def step(r, inp):
        qc, kc, vc = inp
        s = jnp.einsum('bihd,bjhd->bhij', qc, kc) * dmat[None]
        inner = jnp.einsum('bhij,bjhe->bihe', s, vc)
        cross = jnp.einsum('bihd,bhde->bihe', qc, r) * xi[None, :, :, None]
        r = jnp.einsum('bjhd,bjhe->bhde', kc * zeta[None, :, :, None], vc) + g_chunk[None, :, None, None] * r
        return r, inner + cross

    r, o = lax.scan(step, state0.astype(jnp.float32), (to_chunks(q), to_chunks(k), to_chunks(v)))
    return jnp.moveaxis(o, 0, 1).reshape(b, t, h, v.shape[-1]), r


def even_merge(o_a, o_b, g_b, gn, w_out):
    b, t = o_b.shape[:2]
    mu = jnp.mean(o_b, -1, keepdims=True)
    var = jnp.mean(jnp.square(o_b - mu), -1, keepdims=True)
    yb = ((o_b - mu) * lax.rsqrt(var + EPS)).reshape(b, t, H_B * DV_B) * gn.astype(jnp.float32)
    yb = jax.nn.silu(g_b.astype(jnp.float32)) * yb
    return jnp.concatenate([o_a, yb.astype(o_a.dtype)], -1) @ w_out


def mla_project(h, w_in, q_norm, kv_norm, w_uq, pos):
    b, t, _ = h.shape
    cq, ckv, kr = _split(h @ w_in, (D_CQ, D_C, DR_C))
    cq = rmsnorm(cq, q_norm)
    ckv = rmsnorm(ckv, kv_norm)
    qf = (cq @ w_uq).reshape(b, t, H_C, DN_C + DR_C)
    q_nope = qf[..., :DN_C]
    q_rope = rope(qf[..., DN_C:], pos, DR_C, ROPE_THETA)
    k_rope = rope(kr.reshape(b, t, 1, DR_C), pos, DR_C, ROPE_THETA)[:, :, 0]
    return q_nope, q_rope, ckv, k_rope


def mla_prompt(q_nope, q_rope, ckv, k_rope, w_ukv, pos):
    b, t = q_nope.shape[:2]
    kv = (ckv @ w_ukv).reshape(b, t, H_C, DN_C + DV_C)
    k_nope, v = kv[..., :DN_C], kv[..., DN_C:]
    nb = t // QBLK
    scale = (DN_C + DR_C) ** -0.5
    kpos = jnp.arange(t)

    def one(args):
        qn, qr, p_ = args
        s = (jnp.einsum('bqhd,bkhd->bhqk', qn, k_nope, preferred_element_type=jnp.float32)
             + jnp.einsum('bqhr,bkr->bhqk', qr, k_rope, preferred_element_type=jnp.float32)) * scale
        s = jnp.where((kpos[None, :] <= p_[:, None])[None, None], s, -jnp.inf)
        pr = jax.nn.softmax(s, -1)
        return jnp.einsum('bhqk,bkhd->bqhd', pr.astype(v.dtype), v)

    o = lax.map(one, (_blocks(q_nope, nb), _blocks(q_rope, nb), pos.reshape(nb, QBLK)))
    return jnp.moveaxis(o, 0, 1).reshape(b, t, H_C * DV_C)


def mla_sample(q_nope, q_rope, ckv, k_rope, w_ukv, pos, pool_lat, pool_kr, li, page_table):
    db, q = q_nope.shape[:2]
    lat = jnp.concatenate([pool_lat[li, page_table].reshape(db, PAST_LEN, D_C), ckv.astype(pool_lat.dtype)], 1)
    kr = jnp.concatenate([pool_kr[li, page_table].reshape(db, PAST_LEN, DR_C), k_rope.astype(pool_kr.dtype)], 1)
    w = w_ukv.reshape(D_C, H_C, DN_C + DV_C)
    w_uk, w_uv = w[..., :DN_C], w[..., DN_C:]
    q_lat = jnp.einsum('bqhn,chn->bqhc', q_nope, w_uk)
    s = (jnp.einsum('bqhc,bkc->bhqk', q_lat, lat, preferred_element_type=jnp.float32)
         + jnp.einsum('bqhr,bkr->bhqk', q_rope, kr, preferred_element_type=jnp.float32)) * ((DN_C + DR_C) ** -0.5)
    kpos = jnp.arange(lat.shape[1])
    s = jnp.where((kpos[None, :] <= pos[:, None])[None, None], s, -jnp.inf)
    p = jax.nn.softmax(s, -1)
    o_lat = jnp.einsum('bhqk,bkc->bqhc', p.astype(lat.dtype), lat)
    return jnp.einsum('bqhc,chv->bqhv', o_lat, w_uv).reshape(db, q, H_C * DV_C)


def conv_ffn(h, w_up, conv_w, conv_b, w_down, conv_state):
    u = h @ w_up
    t = u.shape[1]
    ext = jnp.concatenate([conv_state.astype(u.dtype), u], 1)
    c = conv_b
    for j in range(CONV_W):
        c = c + ext[:, j:j + t] * conv_w[j]
    y = jax.nn.silu(c[..., :D_FF]) * c[..., D_FF:]
    return y @ w_down, ext[:, t:]


def setup_inputs(seed: int = 0) -> dict:
    key = jax.random.key(seed)
    ks = jax.random.split(key, 26)
    n_pages = PAST_LEN // PAGE_SIZE
    n_pool = (DEC_BATCH * n_pages * 5) // 4
    nrm = lambda k, shape, s: s * jax.random.normal(k, shape, jnp.float32)
    wt = lambda k, shape: nrm(k, shape, shape[-2] ** -0.5)
    gain = lambda k, shape: 1.0 + nrm(k, shape, 0.02)
    page_table = jax.random.permutation(ks[9], n_pool)[:DEC_BATCH * n_pages].reshape(DEC_BATCH, n_pages).astype(jnp.int32)
    return {
        'x_prompt': nrm(ks[0], (BATCH, SEQ, D_MODEL), 1.0),
        'x_sample': nrm(ks[1], (DEC_BATCH, DEC_SEQ, D_MODEL), 1.0),
        'cache_a_k': nrm(ks[2], (N_EVEN, n_pool, PAGE_SIZE, KV_A, HD_A), 1.0),
        'cache_a_v': nrm(ks[3], (N_EVEN, n_pool, PAGE_SIZE, KV_A, HD_A), 1.0),
        'cache_a_kidx': nrm(ks[4], (N_EVEN, n_pool, PAGE_SIZE, D_IDX), 1.0),
        'state_ret': nrm(ks[5], (N_EVEN, DEC_BATCH, H_B, DK_B, DV_B), 0.5),
        'cache_c_lat': nrm(ks[6], (N_ODD, n_pool, PAGE_SIZE, D_C), 1.0),
        'cache_c_krope': nrm(ks[7], (N_ODD, n_pool, PAGE_SIZE, DR_C), 1.0),
        'state_conv': nrm(ks[8], (DEPTH, DEC_BATCH, CONV_W - 1, 2 * D_FF), 1.0),
        'page_table': page_table,
        'norm_mix': gain(ks[10], (DEPTH, D_MODEL)),
        'w_in_even': wt(ks[11], (N_EVEN, D_MODEL, W_EVEN)),
        'ret_gn': gain(ks[12], (N_EVEN, H_B * DV_B)),
        'w_out_even': wt(ks[13], (N_EVEN, H_A * HD_A + H_B * DV_B, D_MODEL)),
        'w_in_odd': wt(ks[14], (N_ODD, D_MODEL, W_ODD)),
        'q_norm_c': gain(ks[15], (N_ODD, D_CQ)),
        'kv_norm_c': gain(ks[16], (N_ODD, D_C)),
        'w_uq': wt(ks[17], (N_ODD, D_CQ, H_C * (DN_C + DR_C))),
        'w_ukv': wt(ks[18], (N_ODD, D_C, H_C * (DN_C + DV_C))),
        'w_out_odd': wt(ks[19], (N_ODD, H_C * DV_C, D_MODEL)),
        'norm_ffn': gain(ks[20], (DEPTH, D_MODEL)),
        'w_up': wt(ks[21], (DEPTH, D_MODEL, 2 * D_FF)),
        'conv_w': wt(ks[22], (DEPTH, CONV_W, 2 * D_FF)),
        'conv_b': nrm(ks[23], (DEPTH, 2 * D_FF), 0.01),
        'w_down': wt(ks[24], (DEPTH, D_FF, D_MODEL)),
        'norm_final': gain(ks[25], (D_MODEL,)),
    }


def reference(x_prompt, x_sample, cache_a_k, cache_a_v, cache_a_kidx, state_ret,
              cache_c_lat, cache_c_krope, state_conv, page_table,
              norm_mix, w_in_even, ret_gn, w_out_even, w_in_odd, q_norm_c, kv_norm_c,
              w_uq, w_ukv, w_out_odd, norm_ffn, w_up, conv_w, conv_b, w_down, norm_final):
    bp, tp = x_prompt.shape[:2]
    ts = x_sample.shape[1]
    pos_p = jnp.arange(tp)
    pos_s = PAST_LEN + jnp.arange(ts)
    xp, xs = x_prompt, x_sample
    a_k_p, a_k_s, a_v_p, a_v_s, a_i_p, a_i_s = [], [], [], [], [], []
    ret_p, ret_s, lat_p, lat_s, kr_p, kr_s, conv_p, conv_s = [], [], [], [], [], [], [], []
    for layer in range(DEPTH):
        li = layer // 2
        hp = rmsnorm(xp, norm_mix[layer])
        hs = rmsnorm(xs, norm_mix[layer])
        if layer % 2 == 0:
            (qa, ka, va, qi, ki, wi), (qb, kb, vb, gb) = even_project(hp, w_in_even[li], pos_p)
            oa = dsa_prompt(qa, ka, va, qi, ki, wi, pos_p)
            ob, rfin = retention(qb, kb, vb, jnp.zeros((bp, H_B, DK_B, DV_B), jnp.float32))
            xp = xp + even_merge(oa, ob, gb, ret_gn[li], w_out_even[li])
            a_k_p.append(ka); a_v_p.append(va); a_i_p.append(ki); ret_p.append(rfin.astype(xp.dtype))
            (qa, ka, va, qi, ki, wi), (qb, kb, vb, gb) = even_project(hs, w_in_even[li], pos_s)
            oa = dsa_sample(qa, ka, va, qi, ki, wi, pos_s, cache_a_k, cache_a_v, cache_a_kidx, li, page_table)
            ob, rfin = retention(qb, kb, vb, state_ret[li])
            xs = xs + even_merge(oa, ob, gb, ret_gn[li], w_out_even[li])
            a_k_s.append(ka); a_v_s.append(va); a_i_s.append(ki); ret_s.append(rfin.astype(xs.dtype))
        else:
            qn, qr, ckv, kr = mla_project(hp, w_in_odd[li], q_norm_c[li], kv_norm_c[li], w_uq[li], pos_p)
            xp = xp + mla_prompt(qn, qr, ckv, kr, w_ukv[li], pos_p) @ w_out_odd[li]
            lat_p.append(ckv); kr_p.append(kr)
            qn, qr, ckv, kr = mla_project(hs, w_in_odd[li], q_norm_c[li], kv_norm_c[li], w_uq[li], pos_s)
            xs = xs + mla_sample(qn, qr, ckv, kr, w_ukv[li], pos_s, cache_c_lat, cache_c_krope, li, page_table) @ w_out_odd[li]
            lat_s.append(ckv); kr_s.append(kr)
        fp, cp = conv_ffn(rmsnorm(xp, norm_ffn[layer]), w_up[layer], conv_w[layer], conv_b[layer], w_down[layer],
                          jnp.zeros((bp, CONV_W - 1, 2 * D_FF), xp.dtype))
        xp = xp + fp
        fs, cs = conv_ffn(rmsnorm(xs, norm_ffn[layer]), w_up[layer], conv_w[layer], conv_b[layer], w_down[layer],
                          state_conv[layer])
        xs = xs + fs
        conv_p.append(cp); conv_s.append(cs)
    y_prompt = rmsnorm(xp, norm_final)
    y_sample = rmsnorm(xs, norm_final)
    st = lambda lst: jnp.stack(lst, 0)
    return (y_prompt, y_sample, st(a_k_p), st(a_k_s), st(a_v_p), st(a_v_s), st(a_i_p), st(a_i_s),
            st(ret_p), st(ret_s), st(lat_p), st(lat_s), st(kr_p), st(kr_s), st(conv_p), st(conv_s))
```

```python
import functools
import math

import numpy as np
import jax
import jax.numpy as jnp
from jax import lax
from jax.experimental import pallas as pl
from jax.experimental.pallas import tpu as pltpu

F32, BF16, I32 = jnp.float32, jnp.bfloat16, jnp.int32

D_MODEL = 1024
PAGE_SIZE = 128
H_A, KV_A, HD_A = 8, 4, 64
ROT_A = HD_A // 4
H_IDX, D_IDX = 4, 64
TOPK_MAX = 256
H_B, DK_B, DV_B = 4, 128, 128
RET_CHUNK = 128
RET_THETA = 10000.0
H_C, DN_C, DR_C, DV_C = 16, 64, 32, 64
D_CQ, D_C = 384, 256
D_FF = 2816
ROPE_THETA = 500000.0
EPS = 1e-6
QBLK = 128
EVEN_SIZES = (H_A * HD_A, KV_A * HD_A, KV_A * HD_A, H_IDX * D_IDX, D_IDX, H_IDX,
              H_B * DK_B, H_B * DK_B, H_B * DV_B, H_B * DV_B)

LANES = 128
NEG = -1e30
INT_MIN = -2 ** 31
VMEM_LIMIT = 56 * 1024 * 1024
FF_CHUNK = 256


def _cparams(sem, vmem=VMEM_LIMIT):
    return pltpu.CompilerParams(dimension_semantics=sem, vmem_limit_bytes=vmem)


def _resident(shape, index_map):
    return pl.BlockSpec(shape, index_map, pipeline_mode=pl.Buffered(1))


def _dot(a, b):
    return jnp.dot(a, b, preferred_element_type=F32)


def _dot_nt(a, b):
    return lax.dot_general(a, b, (((1,), (1,)), ((), ())), preferred_element_type=F32)


def _dot_tn(a, b):
    return lax.dot_general(a, b, (((0,), (0,)), ((), ())), preferred_element_type=F32)


def _silu(x):
    return x * (1.0 / (1.0 + jnp.exp(-x)))


def _rms(xf, g):
    return xf * lax.rsqrt(jnp.mean(xf * xf, -1, keepdims=True) + EPS) * g


def _rope_tables(pos, head_period, rot_start, rot, theta, ident_from=None):
    half = rot // 2
    pos = np.asarray(pos, np.float64)
    inv = theta ** (-np.arange(half, dtype=np.float64) * 2.0 / rot)
    ang = pos[:, None] * inv[None, :]
    cos, sin = np.cos(ang), np.sin(ang)
    n = len(pos)
    c = np.ones((n, LANES)); sp = np.zeros((n, LANES)); sm = np.zeros((n, LANES))
    for lane in range(LANES):
        if ident_from is not None and lane >= ident_from:
            continue
        d = lane % head_period - rot_start
        if 0 <= d < half:
            c[:, lane] = cos[:, d]; sm[:, lane] = -sin[:, d]
        elif half <= d < rot:
            c[:, lane] = cos[:, d - half]; sp[:, lane] = sin[:, d - half]
    f32 = lambda a: np.asarray(a, np.float32)
    if half == LANES // 2:
        return (f32(c), f32(sp + sm))
    return (f32(c), f32(sp), f32(sm))


def _proj_body(*refs, segs, tab_sizes, tab_shifts, n_rg, has_gain):
    it = iter(refs)
    x_ref = next(it)
    gain_ref = next(it) if has_gain else None
    w_ref = next(it)
    tabs = [[next(it) for _ in range(k)] for k in tab_sizes]
    rgs = [next(it) for _ in range(n_rg)]
    outs = list(it)
    x = x_ref[...]
    if has_gain:
        x = _rms(x.astype(F32), gain_ref[...])
    hb = x.astype(BF16)
    for sg in segs:
        off, width = sg["off"], sg["width"]
        o_ref, oo = outs[sg["out"]], sg.get("out_off", 0)
        acc = _dot(hb, w_ref[:, off:off + width])
        kind = sg.get("kind", "none")
        if kind == "rope":
            t = sg["tab"]
            shift = tab_shifts[t]
            cst = tabs[t][0][...]
            sp = tabs[t][1][...]
            sm = tabs[t][2][...] if len(tabs[t]) == 3 else None
            for c in range(width // LANES):
                xc = acc[:, c * LANES:(c + 1) * LANES]
                oc = xc * cst + pltpu.roll(xc, shift, 1) * sp
                if sm is not None:
                    oc = oc + pltpu.roll(xc, LANES - shift, 1) * sm
                if sg.get("scale") is not None:
                    oc = oc * sg["scale"]
                o_ref[:, oo + c * LANES:oo + (c + 1) * LANES] = oc
        elif kind == "rms":
            o_ref[:, oo:oo + width] = _rms(acc, rgs[sg["rg"]][...])
        else:
            o_ref[:, oo:oo + width] = acc


def _proj(x, gain, w, segs, out_widths, tabs, tab_shifts, rms_gains, tm):
    m, k = x.shape
    assert m % tm == 0
    ins, specs = [x], [pl.BlockSpec((tm, k), lambda i: (i, 0))]
    if gain is not None:
        ins.append(gain.reshape(1, k)); specs.append(pl.BlockSpec((1, k), lambda i: (0, 0)))
    ins.append(w); specs.append(_resident(w.shape, lambda i: (0, 0)))
    for tab in tabs:
        nper = tab[0].shape[0] // tm
        for arr in tab:
            ins.append(arr)
            specs.append(pl.BlockSpec((tm, LANES), lambda i, nper=nper: (i % nper, 0)))
    for g in rms_gains:
        ins.append(g.reshape(1, -1)); specs.append(pl.BlockSpec((1, g.shape[-1]), lambda i: (0, 0)))
    body = functools.partial(_proj_body, segs=segs, tab_sizes=[len(t) for t in tabs],
                             tab_shifts=tab_shifts, n_rg=len(rms_gains), has_gain=gain is not None)
    return pl.pallas_call(
        body, grid=(m // tm,), in_specs=specs,
        out_specs=[pl.BlockSpec((tm, wd), lambda i: (i, 0)) for wd in out_widths],
        out_shape=[jax.ShapeDtypeStruct((m, wd), F32) for wd in out_widths],
        compiler_params=_cparams(("parallel",)), name=f"proj_{m}_{k}_{w.shape[1]}")(*ins)


def _mmres_body(*refs, n):
    res_ref, o_ref = refs[0], refs[-1]
    acc = res_ref[...]
    for j in range(n):
        acc = acc + _dot(refs[1 + 2 * j][...].astype(BF16), refs[2 + 2 * j][...])
    o_ref[...] = acc


def _mm_res(res, pairs, tm):
    m, n_out = res.shape
    ins, specs = [res], [pl.BlockSpec((tm, n_out), lambda i: (i, 0))]
    for a, w in pairs:
        ins += [a, w]
        specs += [pl.BlockSpec((tm, a.shape[1]), lambda i: (i, 0)), pl.BlockSpec(w.shape, lambda i: (0, 0))]
    return pl.pallas_call(
        functools.partial(_mmres_body, n=len(pairs)), grid=(m // tm,), in_specs=specs,
        out_specs=pl.BlockSpec((tm, n_out), lambda i: (i, 0)),
        out_shape=jax.ShapeDtypeStruct((m, n_out), F32),
        compiler_params=_cparams(("parallel",)), name=f"outproj_{m}_{len(pairs)}")(*ins)


def _ffn_body(*refs, tm, seq_tiles, per_row_halo, has_final):
    it = iter(refs)
    x_ref, g_ref, wup_ref, cw_ref, cb_ref, wdn_ref = (next(it) for _ in range(6))
    h0_ref, h1_ref = next(it), next(it)
    fg_ref = next(it) if has_final else None
    o_ref, u_ref = next(it), next(it)
    y_ref = next(it) if has_final else None
    acc_scr = next(it)
    carry_scr = None if per_row_halo else next(it)

    i = pl.program_id(0)
    x = x_ref[...]
    hb = _rms(x, g_ref[...]).astype(BF16)
    row = lax.broadcasted_iota(I32, (tm, 1), 0)
    if per_row_halo:
        t_in = row % 8
        is0, is1 = t_in == 0, t_in == 1
    else:
        is0, is1 = row == 0, row == 1
        first = (i % seq_tiles) == 0
    acc_scr[...] = jnp.zeros_like(acc_scr)
    for c in range(D_FF // FF_CHUNK):
        halves = []
        for half in range(2):
            lo = half * D_FF + c * FF_CHUNK
            u = _dot(hb, wup_ref[:, lo:lo + FF_CHUNK])
            if per_row_halo:
                p0 = h0_ref[:, lo:lo + FF_CHUNK]
                p1 = h1_ref[:, lo:lo + FF_CHUNK]
                u_ref[:, lo:lo + FF_CHUNK] = u
            else:
                p0 = jnp.where(first, h0_ref[0, 0:1, lo:lo + FF_CHUNK], carry_scr[0:1, lo:lo + FF_CHUNK])
                p1 = jnp.where(first, h0_ref[0, 1:2, lo:lo + FF_CHUNK], carry_scr[1:2, lo:lo + FF_CHUNK])
                carry_scr[:, lo:lo + FF_CHUNK] = u[tm - 2:tm, :]
                u_ref[0, :, lo:lo + FF_CHUNK] = u[tm - 2:tm, :]
            prev1 = jnp.where(is0, p1, pltpu.roll(u, 1, 0))
            prev2 = jnp.where(is0, p0, jnp.where(is1, p1, pltpu.roll(u, 2, 0)))
            cv = (cb_ref[:, lo:lo + FF_CHUNK] + prev2 * cw_ref[0:1, lo:lo + FF_CHUNK]
                  + prev1 * cw_ref[1:2, lo:lo + FF_CHUNK] + u * cw_ref[2:3, lo:lo + FF_CHUNK])
            halves.append(cv)
        y = (_silu(halves[0]) * halves[1]).astype(BF16)
        acc_scr[...] += _dot(y, wdn_ref[c * FF_CHUNK:(c + 1) * FF_CHUNK, :])
    out = x + acc_scr[...]
    o_ref[...] = out
    if has_final:
        y_ref[...] = _rms(out, fg_ref[...])


def _conv_ffn(x, gain, w_up, conv_w, conv_b, w_down, halo0, halo1, seq_len, tm, final_gain):
    m, d = x.shape
    per_row = seq_len < tm
    seq_tiles = max(seq_len // tm, 1)
    const = lambda i: (0, 0)
    ins = [x, gain.reshape(1, d), w_up, conv_w, conv_b.reshape(1, -1), w_down]
    specs = [pl.BlockSpec((tm, d), lambda i: (i, 0)), pl.BlockSpec((1, d), const),
             _resident(w_up.shape, const), pl.BlockSpec(conv_w.shape, const),
             pl.BlockSpec((1, 2 * D_FF), const), _resident(w_down.shape, const)]
    if per_row:
        ins += [halo0, halo1]
        specs += [pl.BlockSpec((tm, 2 * D_FF), lambda i: (i, 0))] * 2
        u_shape, u_spec = (m, 2 * D_FF), pl.BlockSpec((tm, 2 * D_FF), lambda i: (i, 0))
    else:
        ins += [halo0, halo0]
        specs += [pl.BlockSpec((1, 2, 2 * D_FF), lambda i: (i // seq_tiles, 0, 0))] * 2
        u_shape = (m // seq_len, 2, 2 * D_FF)
        u_spec = pl.BlockSpec((1, 2, 2 * D_FF), lambda i: (i // seq_tiles, 0, 0))
    out_shape = [jax.ShapeDtypeStruct((m, d), F32), jax.ShapeDtypeStruct(u_shape, F32)]
    out_specs = [pl.BlockSpec((tm, d), lambda i: (i, 0)), u_spec]
    if final_gain is not None:
        ins.append(final_gain.reshape(1, d)); specs.append(pl.BlockSpec((1, d), const))
        out_shape.append(jax.ShapeDtypeStruct((m, d), F32))
        out_specs.append(pl.BlockSpec((tm, d), lambda i: (i, 0)))
    scratch = [pltpu.VMEM((tm, d), F32)]
    if not per_row:
        scratch.append(pltpu.VMEM((2, 2 * D_FF), F32))
    body = functools.partial(_ffn_body, tm=tm, seq_tiles=seq_tiles, per_row_halo=per_row,
                             has_final=final_gain is not None)
    return pl.pallas_call(body, grid=(m // tm,), in_specs=specs, out_specs=out_specs, out_shape=out_shape,
                          scratch_shapes=scratch, compiler_params=_cparams(("arbitrary",)),
                          name=f"convffn_{m}")(*ins)


def _ret_body(*refs, nseq, seq_len, chunk, has_state):
    it = iter(refs)
    q_ref, k_ref, v_ref, g_ref = (next(it) for _ in range(4))
    st_ref = next(it) if has_state else None
    dm_ref, xi_ref, zt_ref, gc_ref, gn_ref = (next(it) for _ in range(5))
    y_ref, so_ref, r_scr = next(it), next(it), next(it)
    nch = seq_len // chunk
    dmat = dm_ref[0]
    xi, zeta, gch, gn = xi_ref[0], zt_ref[0], gc_ref[0, 0:1, :], gn_ref[...]

    def seq_body(s, carry):
        r_scr[...] = st_ref[s, 0] if has_state else jnp.zeros_like(r_scr)

        def chunk_body(j, carry2):
            r0 = pl.multiple_of(s * seq_len + j * chunk, 8)
            rows = pl.ds(r0, chunk)
            qc, kc, vc = q_ref[rows, :], k_ref[rows, :], v_ref[rows, :]
            qb, kb, vb = qc.astype(BF16), kc.astype(BF16), vc.astype(BF16)
            sc = _dot_nt(qb, kb) * dmat
            inner = _dot(sc.astype(BF16), vb)
            r = r_scr[...]
            cross = _dot(qb, r.astype(BF16)) * xi
            o = inner + cross
            r_scr[...] = _dot_tn((kc * zeta).astype(BF16), vb) + gch * r
            mu = jnp.mean(o, -1, keepdims=True)
            var = jnp.mean(jnp.square(o - mu), -1, keepdims=True)
            yb = (o - mu) * lax.rsqrt(var + EPS) * gn
            y_ref[rows, :] = _silu(g_ref[rows, :]) * yb
            return carry2

        lax.fori_loop(0, nch, chunk_body, 0)
        so_ref[s, 0] = r_scr[...]
        return carry

    lax.fori_loop(0, nseq, seq_body, 0)


def _retention(q, k, v, g, state0, gn, seq_len, nseq_blk):
    m = q.shape[0]
    nb = m // seq_len
    chunk = math.gcd(seq_len, RET_CHUNK)
    lg = np.log1p(-(2.0 ** (-5.0 - np.arange(H_B, dtype=np.float64))))
    ii = np.arange(chunk, dtype=np.float64)
    diff = ii[:, None] - ii[None, :]
    dmat = np.where(diff >= 0, np.exp(lg[:, None, None] * np.maximum(diff, 0.0)), 0.0)
    xi = np.broadcast_to(np.exp(lg[:, None] * (ii[None, :] + 1.0))[:, :, None], (H_B, chunk, LANES))
    zeta = np.broadcast_to(np.exp(lg[:, None] * (chunk - 1.0 - ii[None, :]))[:, :, None], (H_B, chunk, LANES))
    gch = np.broadcast_to(np.exp(lg * chunk)[:, None, None], (H_B, 8, LANES))
    rows = nseq_blk * seq_len
    tok = pl.BlockSpec((rows, LANES), lambda i, h: (i, h))
    ins, specs = [q, k, v, g], [tok] * 4
    if state0 is not None:
        ins.append(state0)
        specs.append(pl.BlockSpec((nseq_blk, 1, DK_B, DV_B), lambda i, h: (i, h, 0, 0)))
    ins += [jnp.asarray(dmat, F32), jnp.asarray(xi, F32), jnp.asarray(zeta, F32), jnp.asarray(gch, F32),
            gn.reshape(1, -1)]
    specs += [pl.BlockSpec((1, chunk, chunk), lambda i, h: (h, 0, 0)),
              pl.BlockSpec((1, chunk, LANES), lambda i, h: (h, 0, 0)),
              pl.BlockSpec((1, chunk, LANES), lambda i, h: (h, 0, 0)),
              pl.BlockSpec((1, 8, LANES), lambda i, h: (h, 0, 0)),
              pl.BlockSpec((1, LANES), lambda i, h: (0, h))]
    body = functools.partial(_ret_body, nseq=nseq_blk, seq_len=seq_len, chunk=chunk, has_state=state0 is not None)
    return pl.pallas_call(
        body, grid=(nb // nseq_blk, H_B), in_specs=specs,
        out_specs=[tok, pl.BlockSpec((nseq_blk, 1, DK_B, DV_B), lambda i, h: (i, h, 0, 0))],
        out_shape=[jax.ShapeDtypeStruct((m, H_B * DV_B), F32), jax.ShapeDtypeStruct((nb, H_B, DK_B, DV_B), F32)],
        scratch_shapes=[pltpu.VMEM((DK_B, DV_B), F32)],
        compiler_params=_cparams(("parallel", "parallel")), name=f"retention_{m}")(*ins)


def _sortable(score):
    score = jnp.where(score == 0.0, 0.0, score)
    bits = lax.bitcast_convert_type(score, I32)
    return jnp.where(bits >= 0, bits, bits ^ 0x7FFFFFFF)


def _dsa_prompt_body(kidx_ref, ka_ref, vat_ref, qit_ref, wit_ref, qat_ref, o_ref, key_scr, bias_scr, lg_scr,
                     *, seq, topk, kc):
    i = pl.program_id(1)
    nch = ((i + 1) * QBLK + kc - 1) // kc
    qpos = i * QBLK + lax.broadcasted_iota(I32, (1, QBLK), 1)
    krow = lax.broadcasted_iota(I32, (kc, QBLK), 0)
    idx_scale = (H_IDX * D_IDX) ** -0.5

    def score_body(c, carry):
        r0 = pl.multiple_of(c * kc, kc)
        kx = kidx_ref[0, pl.ds(r0, kc), :]
        sc = jnp.zeros((kc, QBLK), F32)
        for h in range(H_IDX):
            s = _dot(kx, qit_ref[0, h * D_IDX:(h + 1) * D_IDX, :])
            sc = sc + jnp.maximum(s, 0.0) * wit_ref[0, h:h + 1, :]
        key = _sortable(sc * idx_scale)
        key_scr[pl.ds(r0, kc), :] = jnp.where(krow + r0 <= qpos, key, INT_MIN)
        return carry

    lax.fori_loop(0, nch, score_body, 0)

    def count(pred):
        def body(c, acc):
            r0 = pl.multiple_of(c * kc, kc)
            ind = pred(key_scr[pl.ds(r0, kc), :], krow + r0)
            return acc + jnp.sum(ind.reshape(kc // 8, 8, QBLK), axis=0)
        acc = lax.fori_loop(0, nch, body, jnp.zeros((8, QBLK), I32))
        return jnp.sum(acc, axis=0, keepdims=True)

    thr = jnp.full((1, QBLK), INT_MIN, I32)
    one = lambda cond: jnp.where(cond, 1, 0)
    cnt = count(lambda k, kp: one(k >= 0))
    thr = jnp.where(cnt >= topk, 0, thr)

    def thr_body(b, t):
        cand = t | lax.shift_left(jnp.int32(1), 30 - b)
        cnt = count(lambda k, kp: one(k >= cand))
        return jnp.where(cnt >= topk, cand, t)

    thr = lax.fori_loop(0, 31, thr_body, thr)
    need = topk - count(lambda k, kp: one(k > thr))

    nbits = max(1, (seq - 1).bit_length())

    def cut_body(b, cut):
        cand = cut | lax.shift_left(jnp.int32(1), nbits - 1 - b)
        f = count(lambda k, kp: jnp.where(k == thr, one(kp < cand), 0))
        return jnp.where(f < need, cand, cut)

    cut = lax.fori_loop(0, nbits, cut_body, jnp.zeros((1, QBLK), I32))

    def bias_body(c, carry):
        r0 = pl.multiple_of(c * kc, kc)
        k = key_scr[pl.ds(r0, kc), :]
        kp = krow + r0
        sel = jnp.where(k > thr, 0.0, jnp.where(k == thr, jnp.where(kp <= cut, 0.0, NEG), NEG))
        bias_scr[pl.ds(r0, kc), :] = jnp.where(kp <= qpos, sel, NEG)
        return carry

    lax.fori_loop(0, nch, bias_body, 0)

    att_scale = HD_A ** -0.5
    rep = H_A // KV_A
    zero_blk = jnp.zeros((HD_A, rep * QBLK), BF16)
    for g in range(KV_A):
        qg = jnp.concatenate([qat_ref[0, (g * rep + r) * HD_A:(g * rep + r + 1) * HD_A, :] for r in range(rep)], axis=1)
        wq = jnp.concatenate([qg if gg == g else zero_blk for gg in range(KV_A)], axis=0)

        def logit_body(c, m):
            r0 = pl.multiple_of(c * kc, kc)
            b = bias_scr[pl.ds(r0, kc), :]
            lgt = _dot(ka_ref[0, pl.ds(r0, kc), :], wq) * att_scale + jnp.concatenate([b] * rep, axis=1)
            lg_scr[pl.ds(r0, kc), :] = lgt
            return jnp.maximum(m, jnp.max(lgt.reshape(kc // 8, 8, rep * QBLK), axis=0))

        m = lax.fori_loop(0, nch, logit_body, jnp.full((8, rep * QBLK), NEG, F32))
        m = jnp.max(m, axis=0, keepdims=True)

        def pv_body(c, carry):
            ssum, acc = carry
            r0 = pl.multiple_of(c * kc, kc)
            p = jnp.exp(lg_scr[pl.ds(r0, kc), :] - m)
            ssum = ssum + jnp.sum(p.reshape(kc // 8, 8, rep * QBLK), axis=0)
            acc = acc + _dot(vat_ref[0, c, g * HD_A:(g + 1) * HD_A, :], p.astype(BF16))
            return ssum, acc

        ssum, acc = lax.fori_loop(0, nch, pv_body, (jnp.zeros((8, rep * QBLK), F32),
                                                   jnp.zeros((HD_A, rep * QBLK), F32)))
        out = acc / jnp.sum(ssum, axis=0, keepdims=True)
        for r in range(rep):
            o_ref[0, (g * rep + r) * HD_A:(g * rep + r + 1) * HD_A, :] = out[:, r * QBLK:(r + 1) * QBLK]


def _dsa_prompt(kidx, ka, vat, qit, wit, qat, seq):
    b = kidx.shape[0]
    kc = min(256, seq)
    topk = min(TOPK_MAX, seq // 4)
    body = functools.partial(_dsa_prompt_body, seq=seq, topk=topk, kc=kc)
    return pl.pallas_call(
        body, grid=(b, seq // QBLK),
        in_specs=[pl.BlockSpec((1, seq, D_IDX), lambda bb, i: (bb, 0, 0)),
                  pl.BlockSpec((1, seq, KV_A * HD_A), lambda bb, i: (bb, 0, 0)),
                  pl.BlockSpec((1, seq // kc, KV_A * HD_A, kc), lambda bb, i: (bb, 0, 0, 0)),
                  pl.BlockSpec((1, H_IDX * D_IDX, QBLK), lambda bb, i: (bb, 0, i)),
                  pl.BlockSpec((1, 8, QBLK), lambda bb, i: (bb, 0, i)),
                  pl.BlockSpec((1, H_A * HD_A, QBLK), lambda bb, i: (bb, 0, i))],
        out_specs=pl.BlockSpec((1, H_A * HD_A, QBLK), lambda bb, i: (bb, 0, i)),
        out_shape=jax.ShapeDtypeStruct((b, H_A * HD_A, seq), F32),
        scratch_shapes=[pltpu.VMEM((seq, QBLK), I32), pltpu.VMEM((seq, QBLK), F32),
                        pltpu.VMEM((seq, (H_A // KV_A) * QBLK), F32)],
        compiler_params=_cparams(("parallel", "parallel")), name="dsa_prompt")(kidx, ka, vat, qit, wit, qat)


def _page_copies(pt_ref, pool_ref, li, b, first_page, n_pages, buf_ref, slot, sem):
    return [pltpu.make_async_copy(pool_ref.at[li, pt_ref[b, first_page + p]],
                                  buf_ref.at[slot, pl.ds(p * PAGE_SIZE, PAGE_SIZE)], sem.at[slot])
            for p in range(n_pages)]


def _dsa_score_body(pt_ref, qi_ref, wi_ref, kin_ref, pool_ref, o_ref, kbuf, sem, *, li, n_pages, ck):
    b = pl.program_id(0)
    nb = pl.num_programs(0)
    slot = b % 2
    past = n_pages * PAGE_SIZE

    @pl.when(b == 0)
    def _():
        for cp in _page_copies(pt_ref, pool_ref, li, 0, 0, n_pages, kbuf, 0, sem):
            cp.start()

    @pl.when(b + 1 < nb)
    def _():
        for cp in _page_copies(pt_ref, pool_ref, li, b + 1, 0, n_pages, kbuf, 1 - slot, sem):
            cp.start()

    for cp in _page_copies(pt_ref, pool_ref, li, b, 0, n_pages, kbuf, slot, sem):
        cp.wait()

    qi = qi_ref[0]
    wi = wi_ref[0]
    idx_scale = (H_IDX * D_IDX) ** -0.5

    def scores(kx):
        s = _dot_nt(qi, kx)
        sc = jnp.zeros((8, kx.shape[0]), F32)
        for h in range(H_IDX):
            sc = sc + jnp.maximum(s[h * 8:(h + 1) * 8, :], 0.0) * wi[:, h:h + 1]
        return sc * idx_scale

    for c in range(past // ck):
        o_ref[0, :, c * ck:(c + 1) * ck] = scores(kbuf[slot, c * ck:(c + 1) * ck, :].astype(BF16))
    o_ref[0, :, past:past + LANES] = scores(kin_ref[0])


def _dsa_scores(page_table, qi, wi, ki_new, pool, li):
    db, n_pages = page_table.shape
    past = n_pages * PAGE_SIZE
    ck = min(2048, past)
    body = functools.partial(_dsa_score_body, li=li, n_pages=n_pages, ck=ck)
    grid_spec = pltpu.PrefetchScalarGridSpec(
        num_scalar_prefetch=1, grid=(db,),
        in_specs=[pl.BlockSpec((1, H_IDX * 8, D_IDX), lambda b, pt: (b, 0, 0)),
                  pl.BlockSpec((1, 8, H_IDX), lambda b, pt: (b, 0, 0)),
                  pl.BlockSpec((1, LANES, D_IDX), lambda b, pt: (b, 0, 0)),
                  pl.BlockSpec(memory_space=pl.ANY)],
        out_specs=pl.BlockSpec((1, 8, past + LANES), lambda b, pt: (b, 0, 0)),
        scratch_shapes=[pltpu.VMEM((2, past, D_IDX), F32), pltpu.SemaphoreType.DMA((2,))])
    return pl.pallas_call(body, grid_spec=grid_spec,
                          out_shape=jax.ShapeDtypeStruct((db, 8, past + LANES), F32),
                          compiler_params=_cparams(("arbitrary",)), name="dsa_scores")(
                              page_table, qi, wi, ki_new, pool)


def _select_body(s_ref, o_ref, *, past, topk, rows):
    nk = s_ref.shape[1]
    key = _sortable(s_ref[...])
    kpos = lax.broadcasted_iota(I32, (rows, nk), 1)
    qpos = past + lax.broadcasted_iota(I32, (rows, 1), 0) % 8
    adm = kpos <= qpos
    key = jnp.where(adm, key, INT_MIN)

    def count(ind):
        return jnp.sum(ind, axis=1, keepdims=True)

    one = lambda cond: jnp.where(cond, 1, 0)
    thr = jnp.full((rows, 1), INT_MIN, I32)
    thr = jnp.where(count(one(key >= 0)) >= topk, 0, thr)

    def thr_body(b, t):
        cand = t | lax.shift_left(jnp.int32(1), 30 - b)
        return jnp.where(count(one(key >= cand)) >= topk, cand, t)

    thr = lax.fori_loop(0, 31, thr_body, thr)
    need = topk - count(one(key > thr))
    nbits = max(1, (nk - 1).bit_length())

    def cut_body(b, cut):
        cand = cut | lax.shift_left(jnp.int32(1), nbits - 1 - b)
        f = count(jnp.where(key == thr, one(kpos < cand), 0))
        return jnp.where(f < need, cand, cut)

    cut = lax.fori_loop(0, nbits, cut_body, jnp.zeros((rows, 1), I32))
    sel = jnp.where(key > thr, 0.0, jnp.where(key == thr, jnp.where(kpos <= cut, 0.0, NEG), NEG))
    o_ref[...] = jnp.where(adm, sel, NEG)


def _select_topk(scores, past, rows):
    r, nk = scores.shape
    topk = min(TOPK_MAX, (past + 8) // 4)
    body = functools.partial(_select_body, past=past, topk=topk, rows=rows)
    return pl.pallas_call(body, grid=(r // rows,),
                          in_specs=[pl.BlockSpec((rows, nk), lambda i: (i, 0))],
                          out_specs=pl.BlockSpec((rows, nk), lambda i: (i, 0)),
                          out_shape=jax.ShapeDtypeStruct((r, nk), F32),
                          compiler_params=_cparams(("parallel",)), name="dsa_select")(scores)


def _online_update(m_scr, l_scr, acc_scr, logits, masked, v):
    m_old = m_scr[...]
    m_new = jnp.maximum(m_old, jnp.max(logits, axis=1, keepdims=True))
    p = jnp.exp(logits - m_new)
    if masked is not None:
        p = jnp.where(masked, 0.0, p)
    alpha = jnp.exp(m_old - m_new)
    l_scr[...] = alpha * l_scr[...] + jnp.sum(p, axis=1, keepdims=True)
    acc_scr[...] = alpha * acc_scr[...] + _dot(p.astype(BF16), v)
    m_scr[...] = m_new


def _dsa_attn_body(pt_ref, q_ref, bias_ref, biasn_ref, kn_ref, vn_ref, kpool_ref, vpool_ref, o_ref,
                   kbuf, vbuf, sem_k, sem_v, m_scr, l_scr, acc_scr, *, li, pc, nch):
    b, c = pl.program_id(0), pl.program_id(1)
    nb = pl.num_programs(0)
    step = b * nch + c
    slot = step % 2
    rows = H_A * 8

    def copies(bb, cc, sl):
        return (_page_copies(pt_ref, kpool_ref, li, bb, cc * pc, pc, kbuf, sl, sem_k)
                + _page_copies(pt_ref, vpool_ref, li, bb, cc * pc, pc, vbuf, sl, sem_v))

    @pl.when(step == 0)
    def _():
        for cp in copies(0, 0, 0):
            cp.start()

    @pl.when(step + 1 < nb * nch)
    def _():
        nxt = step + 1
        for cp in copies(nxt // nch, nxt % nch, 1 - slot):
            cp.start()

    for cp in copies(b, c, slot):
        cp.wait()

    @pl.when(c == 0)
    def _():
        m_scr[...] = jnp.full_like(m_scr, NEG)
        l_scr[...] = jnp.zeros_like(l_scr)
        acc_scr[...] = jnp.zeros_like(acc_scr)

    q = q_ref[0]
    att_scale = HD_A ** -0.5

    def piece(k, v, bias8):
        bias = jnp.concatenate([bias8] * H_A, axis=0)
        logits = _dot_nt(q, k) * att_scale + bias
        _online_update(m_scr, l_scr, acc_scr, logits, bias < 0.0, v)

    piece(kbuf[slot].astype(BF16), vbuf[slot].astype(BF16), bias_ref[0])

    @pl.when(c == nch - 1)
    def _():
        piece(kn_ref[0], vn_ref[0], biasn_ref[0])
        out = acc_scr[...] / l_scr[...]
        rep = H_A // KV_A
        for h in range(H_A):
            g = h // rep
            o_ref[0, :, h * HD_A:(h + 1) * HD_A] = out[h * 8:(h + 1) * 8, g * HD_A:(g + 1) * HD_A]


def _dsa_attn(page_table, qbd, bias, k_new, v_new, pool_k, pool_v, li):
    db, n_pages = page_table.shape
    pc = min(16, n_pages)
    nch = n_pages // pc
    ck = pc * PAGE_SIZE
    width = KV_A * HD_A
    body = functools.partial(_dsa_attn_body, li=li, pc=pc, nch=nch)
    grid_spec = pltpu.PrefetchScalarGridSpec(
        num_scalar_prefetch=1, grid=(db, nch),
        in_specs=[pl.BlockSpec((1, H_A * 8, width), lambda b, c, pt: (b, 0, 0)),
                  pl.BlockSpec((1, 8, ck), lambda b, c, pt: (b, 0, c)),
                  pl.BlockSpec((1, 8, LANES), lambda b, c, pt: (b, 0, n_pages)),
                  pl.BlockSpec((1, LANES, width), lambda b, c, pt: (b, 0, 0)),
                  pl.BlockSpec((1, LANES, width), lambda b, c, pt: (b, 0, 0)),
                  pl.BlockSpec(memory_space=pl.ANY), pl.BlockSpec(memory_space=pl.ANY)],
        out_specs=pl.BlockSpec((1, 8, H_A * HD_A), lambda b, c, pt: (b, 0, 0)),
        scratch_shapes=[pltpu.VMEM((2, ck, width), F32), pltpu.VMEM((2, ck, width), F32),
                        pltpu.SemaphoreType.DMA((2,)), pltpu.SemaphoreType.DMA((2,)),
                        pltpu.VMEM((H_A * 8, 1), F32), pltpu.VMEM((H_A * 8, 1), F32),
                        pltpu.VMEM((H_A * 8, width), F32)])
    return pl.pallas_call(body, grid_spec=grid_spec,
                          out_shape=jax.ShapeDtypeStruct((db, 8, H_A * HD_A), F32),
                          compiler_params=_cparams(("arbitrary", "arbitrary")), name="dsa_attn")(
                              page_table, qbd, bias, bias, k_new, v_new, pool_k, pool_v)


def _mla_prompt_body(q_ref, kv_ref, kr_ref, o_ref, kb_scr, vb_scr, m_scr, l_scr, acc_scr, *, seq, tq):
    i = pl.program_id(2)
    lane = lax.broadcasted_iota(I32, (tq, LANES), 1)

    @pl.when(i == 0)
    def _():
        def fill(c, carry):
            rows = pl.ds(pl.multiple_of(c * tq, tq), tq)
            kr = kr_ref[rows, :]
            for hh in range(2):
                kvh = kv_ref[rows, hh * LANES:(hh + 1) * LANES]
                kb_scr[hh, rows, :] = jnp.where(lane < DN_C, kvh, kr).astype(BF16)
                vb_scr[hh, rows, :] = kvh.astype(BF16)
            return carry
        lax.fori_loop(0, seq // tq, fill, 0)

    scale = (DN_C + DR_C) ** -0.5
    qpos = i * tq + lax.broadcasted_iota(I32, (tq, 1), 0)
    kcol = lax.broadcasted_iota(I32, (tq, tq), 1)
    outs = []
    for hh in range(2):
        q = q_ref[:, hh * LANES:(hh + 1) * LANES].astype(BF16)
        m_scr[...] = jnp.full_like(m_scr, NEG)
        l_scr[...] = jnp.zeros_like(l_scr)
        acc_scr[...] = jnp.zeros_like(acc_scr)

        def kv_step(c, carry):
            rows = pl.ds(pl.multiple_of(c * tq, tq), tq)
            logits = _dot_nt(q, kb_scr[hh, rows, :]) * scale
            masked = kcol + c * tq > qpos
            _online_update(m_scr, l_scr, acc_scr, jnp.where(masked, NEG, logits), masked, vb_scr[hh, rows, :])
            return carry

        lax.fori_loop(0, i + 1, kv_step, 0)
        outs.append(acc_scr[...] / l_scr[...])
    o_ref[...] = jnp.where(lane < DV_C, pltpu.roll(outs[0], DV_C, 1), outs[1])


def _mla_prompt(q_all, kv, krseg, nb, seq):
    m = q_all.shape[0]
    tq = min(256, seq)
    nq = seq // tq
    body = functools.partial(_mla_prompt_body, seq=seq, tq=tq)
    return pl.pallas_call(
        body, grid=(nb, H_C // 2, nq),
        in_specs=[pl.BlockSpec((tq, 2 * LANES), lambda b, hp, i: (b * nq + i, hp)),
                  pl.BlockSpec((seq, 2 * LANES), lambda b, hp, i: (b, hp)),
                  pl.BlockSpec((seq, LANES), lambda b, hp, i: (b, 0))],
        out_specs=pl.BlockSpec((tq, LANES), lambda b, hp, i: (b * nq + i, hp)),
        out_shape=jax.ShapeDtypeStruct((m, H_C * DV_C), F32),
        scratch_shapes=[pltpu.VMEM((2, seq, LANES), BF16), pltpu.VMEM((2, seq, LANES), BF16),
                        pltpu.VMEM((tq, 1), F32), pltpu.VMEM((tq, 1), F32), pltpu.VMEM((tq, LANES), F32)],
        compiler_params=_cparams(("parallel", "parallel", "arbitrary")), name="mla_prompt")(q_all, kv, krseg)


def _mla_sample_body(pt_ref, q_ref, latn_ref, krn_ref, wuk_ref, wuv_ref, lpool_ref, rpool_ref, o_ref,
                     lbuf, rbuf, sem_l, sem_r, ql_scr, qr_scr, m_scr, l_scr, acc_scr, *, li, pc, nch, past):
    b, c = pl.program_id(0), pl.program_id(1)
    nb = pl.num_programs(0)
    step = b * nch + c
    slot = step % 2
    rows = H_C * 8

    def copies(bb, cc, sl):
        return (_page_copies(pt_ref, lpool_ref, li, bb, cc * pc, pc, lbuf, sl, sem_l)
                + _page_copies(pt_ref, rpool_ref, li, bb, cc * pc, pc, rbuf, sl, sem_r))

    @pl.when(step == 0)
    def _():
        for cp in copies(0, 0, 0):
            cp.start()

    @pl.when(step + 1 < nb * nch)
    def _():
        nxt = step + 1
        for cp in copies(nxt // nch, nxt % nch, 1 - slot):
            cp.start()

    @pl.when(c == 0)
    def _():
        m_scr[...] = jnp.full_like(m_scr, NEG)
        l_scr[...] = jnp.zeros_like(l_scr)
        acc_scr[...] = jnp.zeros_like(acc_scr)
        for h in range(H_C):
            qh = q_ref[0, :, h * LANES:(h + 1) * LANES]
            ql_scr[h * 8:(h + 1) * 8, :] = _dot(qh.astype(BF16), wuk_ref[h]).astype(BF16)
            qr_scr[h * 8:(h + 1) * 8, :] = qh[:, DN_C:DN_C + DR_C].astype(BF16)

    for cp in copies(b, c, slot):
        cp.wait()

    scale = (DN_C + DR_C) ** -0.5
    ql, qr = ql_scr[...], qr_scr[...]

    def piece(lat, kr, masked):
        logits = (_dot_nt(ql, lat) + _dot_nt(qr, kr)) * scale
        if masked is not None:
            logits = jnp.where(masked, NEG, logits)
        _online_update(m_scr, l_scr, acc_scr, logits, masked, lat)

    piece(lbuf[slot].astype(BF16), rbuf[slot].astype(BF16), None)

    @pl.when(c == nch - 1)
    def _():
        qrow = lax.broadcasted_iota(I32, (rows, LANES), 0) % 8
        kcol = lax.broadcasted_iota(I32, (rows, LANES), 1)
        piece(latn_ref[0], krn_ref[0], kcol > qrow)
        ol = (acc_scr[...] / l_scr[...]).astype(BF16)
        for hp in range(H_C // 2):
            r = (_dot(ol[(2 * hp) * 8:(2 * hp + 1) * 8, :], wuv_ref[2 * hp])
                 + _dot(ol[(2 * hp + 1) * 8:(2 * hp + 2) * 8, :], wuv_ref[2 * hp + 1]))
            o_ref[0, :, hp * LANES:(hp + 1) * LANES] = r


def _mla_sample(page_table, q_all, lat_new, kr_new, wuk, wuv, pool_lat, pool_kr, li):
    db, n_pages = page_table.shape
    pc = min(16, n_pages)
    nch = n_pages // pc
    ck = pc * PAGE_SIZE
    past = n_pages * PAGE_SIZE
    rows = H_C * 8
    body = functools.partial(_mla_sample_body, li=li, pc=pc, nch=nch, past=past)
    grid_spec = pltpu.PrefetchScalarGridSpec(
        num_scalar_prefetch=1, grid=(db, nch),
        in_specs=[pl.BlockSpec((1, 8, H_C * LANES), lambda b, c, pt: (b, 0, 0)),
                  pl.BlockSpec((1, LANES, D_C), lambda b, c, pt: (b, 0, 0)),
                  pl.BlockSpec((1, LANES, DR_C), lambda b, c, pt: (b, 0, 0)),
                  pl.BlockSpec(wuk.shape, lambda b, c, pt: (0, 0, 0)),
                  pl.BlockSpec(wuv.shape, lambda b, c, pt: (0, 0, 0)),
                  pl.BlockSpec(memory_space=pl.ANY), pl.BlockSpec(memory_space=pl.ANY)],
        out_specs=pl.BlockSpec((1, 8, H_C * DV_C), lambda b, c, pt: (b, 0, 0)),
        scratch_shapes=[pltpu.VMEM((2, ck, D_C), F32), pltpu.VMEM((2, ck, DR_C), F32),
                        pltpu.SemaphoreType.DMA((2,)), pltpu.SemaphoreType.DMA((2,)),
                        pltpu.VMEM((rows, D_C), BF16), pltpu.VMEM((rows, DR_C), BF16),
                        pltpu.VMEM((rows, 1), F32), pltpu.VMEM((rows, 1), F32), pltpu.VMEM((rows, D_C), F32)])
    return pl.pallas_call(body, grid_spec=grid_spec,
                          out_shape=jax.ShapeDtypeStruct((db, 8, H_C * DV_C), F32),
                          compiler_params=_cparams(("arbitrary", "arbitrary")), name="mla_sample")(
                              page_table, q_all, lat_new, kr_new, wuk, wuv, pool_lat, pool_kr)


def _split_cols(w, sizes):
    out, o = [], 0
    for s in sizes:
        out.append(w[:, o:o + s])
        o += s
    return out


def _prep_w_even(w):
    qa, ka, va, qi, ki, wi, qb, kb, vb, gb = _split_cols(w, EVEN_SIZES)
    tail = jnp.concatenate([ki, wi, jnp.zeros((w.shape[0], LANES - D_IDX - H_IDX), w.dtype)], 1)
    return jnp.concatenate([qa, ka, va, qi, qb, kb, vb, gb, tail], 1).astype(BF16)


_EVEN_SEGS = [
    ("qa", 512, "rope", 0), ("ka", 256, "rope", 0), ("va", 256, "none", None), ("qi", 256, "rope", 0),
    ("qb", 512, "rope", 1), ("kb", 512, "rope", 1), ("vb", 512, "none", None), ("gb", 512, "none", None),
    ("tail", 128, "rope", 2)]


def _even_segs():
    segs, off = [], 0
    for j, (name, width, kind, tab) in enumerate(_EVEN_SEGS):
        sg = dict(off=off, width=width, out=j, kind=kind)
        if kind == "rope":
            sg["tab"] = tab
        if name == "kb":
            sg["scale"] = DK_B ** -0.5
        segs.append(sg)
        off += width
    return segs, [s[1] for s in _EVEN_SEGS]


def _tile_rows(tabs, reps):
    return [tuple(np.tile(a, (reps, 1)) for a in tab) for tab in tabs]


def _pad_rows(a, rows):
    return jnp.concatenate([a, jnp.zeros((a.shape[0], rows - a.shape[1]) + a.shape[2:], a.dtype)], 1)


def kernel(x_prompt, x_sample, cache_a_k, cache_a_v, cache_a_kidx, state_ret, cache_c_lat, cache_c_krope,
           state_conv, page_table, norm_mix, w_in_even, ret_gn, w_out_even, w_in_odd, q_norm_c, kv_norm_c,
           w_uq, w_ukv, w_out_odd, norm_ffn, w_up, conv_w, conv_b, w_down, norm_final):
    bp, tp, d = x_prompt.shape
    db, ts, _ = x_sample.shape
    n_pages = page_table.shape[1]
    past = n_pages * PAGE_SIZE
    depth = norm_mix.shape[0]
    mp, ms = bp * tp, db * ts
    tm_p = min(512, tp)
    tm_s = min(128, ms)
    pos_p = np.arange(tp)
    pos_s = past + np.arange(ts)

    def tabs_for(pos):
        return [_rope_tables(pos, HD_A, 0, ROT_A, ROPE_THETA),
                _rope_tables(pos, DK_B, 0, DK_B, RET_THETA),
                _rope_tables(pos, HD_A, 0, ROT_A, ROPE_THETA, ident_from=D_IDX),
                _rope_tables(pos, LANES, DN_C, DR_C, ROPE_THETA)]

    tabs_p = tabs_for(pos_p)
    tabs_s = _tile_rows(tabs_for(pos_s), tm_s // ts)
    shifts = [ROT_A // 2, DK_B // 2, ROT_A // 2, DR_C // 2]

    xp = x_prompt.reshape(mp, d)
    xs = x_sample.reshape(ms, d)
    pool_k = cache_a_k.reshape(cache_a_k.shape[:3] + (KV_A * HD_A,))
    pool_v = cache_a_v.reshape(cache_a_v.shape[:3] + (KV_A * HD_A,))
    out = {k: [] for k in ("akp", "aks", "avp", "avs", "aip", "ais", "rp", "rs", "lp", "ls", "kp", "ks", "cp", "cs")}
    yp = ys = None

    for layer in range(depth):
        li = layer // 2
        if layer % 2 == 0:
            w_in = _prep_w_even(w_in_even[li])
            segs, widths = _even_segs()
            w_o = w_out_even[li].astype(BF16)
            w_oa, w_ob = w_o[:H_A * HD_A], w_o[H_A * HD_A:]
            qa, ka, va, qi, qb, kb, vb, gb, tail = _proj(xp, norm_mix[layer], w_in, segs, widths,
                                                         tabs_p[:3], shifts[:3], [], tm_p)
            t3 = lambda a: a.reshape(bp, tp, -1)
            kc = min(256, tp)
            vat = t3(va).astype(BF16).reshape(bp, tp // kc, kc, KV_A * HD_A).transpose(0, 1, 3, 2)
            wit = _pad_rows(t3(tail[:, D_IDX:D_IDX + H_IDX]).transpose(0, 2, 1), 8)
            o_at = _dsa_prompt(t3(tail[:, :D_IDX]).astype(BF16), t3(ka).astype(BF16), vat,
                               t3(qi).astype(BF16).transpose(0, 2, 1), wit,
                               t3(qa).astype(BF16).transpose(0, 2, 1), tp)
            oa = o_at.transpose(0, 2, 1).reshape(mp, H_A * HD_A)
            yb, rfin = _retention(qb, kb, vb, gb, None, ret_gn[li], tp, 1)
            xp = _mm_res(xp, [(oa, w_oa), (yb, w_ob)], tm_p)
            out["akp"].append(ka.reshape(bp, tp, KV_A, HD_A)); out["avp"].append(va.reshape(bp, tp, KV_A, HD_A))
            out["aip"].append(tail[:, :D_IDX].reshape(bp, tp, D_IDX)); out["rp"].append(rfin)
            qa, ka, va, qi, qb, kb, vb, gb, tail = _proj(xs, norm_mix[layer], w_in, segs, widths,
                                                         tabs_s[:3], shifts[:3], [], tm_s)
            s3 = lambda a: a.reshape(db, ts, -1)
            ki = s3(tail[:, :D_IDX])
            qi_s = s3(qi).reshape(db, ts, H_IDX, D_IDX).transpose(0, 2, 1, 3).reshape(db, H_IDX * ts, D_IDX)
            scores = _dsa_scores(page_table, qi_s.astype(BF16), s3(tail[:, D_IDX:D_IDX + H_IDX]),
                                 _pad_rows(ki, LANES).astype(BF16), cache_a_kidx, li)
            nk = past + LANES
            bias = _select_topk(scores.reshape(db * 8, nk), past, min(64, db * 8)).reshape(db, 8, nk)
            rep = H_A // KV_A
            q4 = s3(qa).reshape(db, ts, H_A, HD_A).transpose(0, 2, 1, 3)
            eye = jnp.asarray(np.repeat(np.eye(KV_A), rep, axis=0), F32)
            qbd = (q4[:, :, :, None, :] * eye[None, :, None, :, None]).reshape(db, H_A * ts, KV_A * HD_A)
            oa = _dsa_attn(page_table, qbd.astype(BF16), bias, _pad_rows(s3(ka), LANES).astype(BF16),
                           _pad_rows(s3(va), LANES).astype(BF16), pool_k, pool_v, li).reshape(ms, H_A * HD_A)
            yb, rfin = _retention(qb, kb, vb, gb, state_ret[li], ret_gn[li], ts, min(16, db))
            xs = _mm_res(xs, [(oa, w_oa), (yb, w_ob)], tm_s)
            out["aks"].append(ka.reshape(db, ts, KV_A, HD_A)); out["avs"].append(va.reshape(db, ts, KV_A, HD_A))
            out["ais"].append(ki); out["rs"].append(rfin)
        else:
            wi_ = w_in_odd[li]
            zc = lambda n: jnp.zeros((d, n), wi_.dtype)
            w_in = jnp.concatenate([wi_[:, :D_CQ + D_C], zc(DN_C), wi_[:, D_CQ + D_C:], zc(LANES - DN_C - DR_C)],
                                   1).astype(BF16)
            segs_in = [dict(off=0, width=D_CQ, out=0, kind="rms", rg=0),
                       dict(off=D_CQ, width=D_C, out=1, kind="rms", rg=1),
                       dict(off=D_CQ + D_C, width=LANES, out=2, kind="rope", tab=0)]
            wq = w_uq[li].reshape(D_CQ, H_C, DN_C + DR_C)
            wq = jnp.concatenate([wq, jnp.zeros((D_CQ, H_C, LANES - DN_C - DR_C), wq.dtype)], 2)
            wq = wq.reshape(D_CQ, H_C * LANES).astype(BF16)
            segs_q = [dict(off=j * 512, width=512, out=0, out_off=j * 512, kind="rope", tab=0) for j in range(4)]
            wkv = w_ukv[li].astype(BF16)
            segs_kv = [dict(off=j * 512, width=512, out=0, out_off=j * 512) for j in range(4)]
            w_o = w_out_odd[li].astype(BF16)
            w3 = w_ukv[li].reshape(D_C, H_C, DN_C + DV_C)
            wuk = jnp.concatenate([w3[..., :DN_C].transpose(1, 2, 0), jnp.zeros((H_C, LANES - DN_C, D_C), F32)],
                                  1).astype(BF16)
            wuv_h = w3[..., DN_C:].transpose(1, 0, 2)
            zv = jnp.zeros_like(wuv_h)
            even = (jnp.arange(H_C) % 2 == 0)[:, None, None]
            wuv = jnp.concatenate([jnp.where(even, wuv_h, zv), jnp.where(even, zv, wuv_h)], 2).astype(BF16)
            cq, ckv, krseg = _proj(xp, norm_mix[layer], w_in, segs_in, [D_CQ, D_C, LANES], [tabs_p[3]], [shifts[3]],
                                   [q_norm_c[li], kv_norm_c[li]], tm_p)
            (q_all,) = _proj(cq, None, wq, segs_q, [H_C * LANES], [tabs_p[3]], [shifts[3]], [], tm_p)
            (kv,) = _proj(ckv, None, wkv, segs_kv, [H_C * LANES], [], [], [], tm_p)
            o = _mla_prompt(q_all, kv, krseg, bp, tp)
            xp = _mm_res(xp, [(o, w_o)], tm_p)
            out["lp"].append(ckv.reshape(bp, tp, D_C)); out["kp"].append(krseg[:, DN_C:DN_C + DR_C].reshape(bp, tp, DR_C))
            cq, ckv, krseg = _proj(xs, norm_mix[layer], w_in, segs_in, [D_CQ, D_C, LANES], [tabs_s[3]], [shifts[3]],
                                   [q_norm_c[li], kv_norm_c[li]], tm_s)
            (q_all,) = _proj(cq, None, wq, segs_q, [H_C * LANES], [tabs_s[3]], [shifts[3]], [], tm_s)
            kr_s = krseg[:, DN_C:DN_C + DR_C].reshape(db, ts, DR_C)
            o = _mla_sample(page_table, q_all.reshape(db, ts, H_C * LANES),
                            _pad_rows(ckv.reshape(db, ts, D_C), LANES).astype(BF16),
                            _pad_rows(kr_s, LANES).astype(BF16), wuk, wuv, cache_c_lat, cache_c_krope, li)
            xs = _mm_res(xs, [(o.reshape(ms, H_C * DV_C), w_o)], tm_s)
            out["ls"].append(ckv.reshape(db, ts, D_C)); out["ks"].append(kr_s)
        last = layer == depth - 1
        fg = norm_final if last else None
        wu, wd = w_up[layer].astype(BF16), w_down[layer].astype(BF16)
        res = _conv_ffn(xp, norm_ffn[layer], wu, conv_w[layer], conv_b[layer], wd,
                        jnp.zeros((bp, 2, 2 * D_FF), F32), None, tp, tm_p, fg)
        xp, cp = res[0], res[1]
        if last:
            yp = res[2]
        st = state_conv[layer]
        res = _conv_ffn(xs, norm_ffn[layer], wu, conv_w[layer], conv_b[layer], wd,
                        jnp.repeat(st[:, 0], ts, axis=0), jnp.repeat(st[:, 1], ts, axis=0), ts, tm_s, fg)
        xs, us = res[0], res[1]
        if last:
            ys = res[2]
        out["cp"].append(cp); out["cs"].append(us.reshape(db, ts, 2 * D_FF)[:, ts - 2:])

    st = lambda lst: jnp.stack(lst, 0)
    return (yp.reshape(bp, tp, d), ys.reshape(db, ts, d), st(out["akp"]), st(out["aks"]), st(out["avp"]),
            st(out["avs"]), st(out["aip"]), st(out["ais"]), st(out["rp"]), st(out["rs"]), st(out["lp"]),
            st(out["ls"]), st(out["kp"]), st(out["ks"]), st(out["cp"]), st(out["cs"]))
```
